```python
import math
import jax, jax.numpy as jnp
from jax import lax
import numpy as np

D_MODEL = 2048
BATCH = 2
SEQ = 4096
DEPTH = 2
DEC_BATCH = 128
DEC_SEQ = 4
PAST_LEN = 8192
PAGE_SIZE = 128

N_META = 16
BLOCK_Q = 128
ROPE_THETA = 500000.0
EPS = 1e-6
NEG_INF = -1e30
N_EVEN = (DEPTH + 1) // 2
N_ODD = DEPTH // 2
HA = D_MODEL // 256
DA = 128
HB = D_MODEL // 256
Q_RANK = 512
KV_RANK = 512
NOPE_D = 128
ROPE_D = 64
V_D = 128
HDC = 64
HC = D_MODEL // HDC
KVC = HC // 8
GC = HC // KVC
ROT_C = HDC // 4
WINDOW = 128
_FF_RAW = -(-8 * D_MODEL // 3)
D_FF = -(-_FF_RAW // 256) * 256
O_FQ = 0
O_FK = HA * DA
O_FV = 2 * HA * DA
O_FG = 3 * HA * DA
O_CQ = O_FG + HA
O_CKV = O_CQ + Q_RANK
O_KPE = O_CKV + KV_RANK
IN_A = O_KPE + ROPE_D
FOX_SCALE = 1.0 / math.sqrt(DA)
MLA_SCALE = 1.0 / math.sqrt(NOPE_D + ROPE_D)
SWA_SCALE = 1.0 / math.sqrt(HDC)

kernel_name = 'fox_mla_swa_hybrid_step'


def rmsnorm(x, g):
    xf = x.astype(jnp.float32)
    y = xf * lax.rsqrt(jnp.mean(xf * xf, axis=-1, keepdims=True) + EPS)
    return (y * g.astype(jnp.float32)).astype(x.dtype)


def rope(x, pos, rot):
    half = rot // 2
    inv_freq = jnp.exp(jnp.arange(half, dtype=jnp.float32) * (-2.0 * math.log(ROPE_THETA) / rot))
    ang = pos.astype(jnp.float32)[:, None] * inv_freq
    shape = (ang.shape[0],) + (1,) * (x.ndim - 3) + (half,)
    cos = jnp.cos(ang).reshape(shape)
    sin = jnp.sin(ang).reshape(shape)
    xf = x[..., :rot].astype(jnp.float32)
    x1, x2 = xf[..., :half], xf[..., half:]
    r = jnp.concatenate([x1 * cos - x2 * sin, x2 * cos + x1 * sin], axis=-1).astype(x.dtype)
    return jnp.concatenate([r, x[..., rot:]], axis=-1)


def swiglu(h, wg, wu, wd):
    return (jax.nn.silu(h @ wg) * (h @ wu)) @ wd


def to_blocks(a, lead):
    a = jnp.pad(a, [(0, 0), (lead, 0)] + [(0, 0)] * (a.ndim - 2))
    nb = a.shape[1] // BLOCK_Q
    a = a.reshape((a.shape[0], nb, BLOCK_Q) + a.shape[2:])
    return jnp.moveaxis(a, 1, 0)


def from_blocks(a, lead):
    a = jnp.moveaxis(a, 0, 1)
    a = a.reshape((a.shape[0], -1) + a.shape[3:])
    return a[:, lead:]


def split_softmax(logits):
    p = jax.nn.softmax(jnp.concatenate(logits, axis=-1), axis=-1)
    parts, off = [], 0
    for s in logits:
        n = s.shape[-1]
        parts.append(p[..., off:off + n])
        off += n
    return parts


def fox_attend(q, cq, qpos, segs):
    cqh = jnp.moveaxis(cq, 2, 1)[..., :, None]
    logits = []
    for k, v, ck, kpos in segs:
        s = jnp.einsum('bqhd,bkhd->bhqk', q, k).astype(jnp.float32) * FOX_SCALE
        s = s + cqh - jnp.moveaxis(ck, 2, 1)[..., None, :]
        logits.append(jnp.where(kpos[None, :] <= qpos[:, None], s, NEG_INF))
    probs = split_softmax(logits)
    outs = [jnp.einsum('bhqk,bkhd->bqhd', p.astype(seg[1].dtype), seg[1]) for p, seg in zip(probs, segs)]
    out = outs[0]
    for o in outs[1:]:
        out = out + o
    return out


def mla_attend(q_lat, q_pe, qpos, segs):
    logits = []
    for ckv, kpe, kpos in segs:
        s = (jnp.einsum('bqhr,bkr->bhqk', q_lat, ckv).astype(jnp.float32)
             + jnp.einsum('bqhp,bkp->bhqk', q_pe, kpe).astype(jnp.float32)) * MLA_SCALE
        logits.append(jnp.where(kpos[None, :] <= qpos[:, None], s, NEG_INF))
    probs = split_softmax(logits)
    outs = [jnp.einsum('bhqk,bkr->bqhr', p.astype(seg[0].dtype), seg[0]) for p, seg in zip(probs, segs)]
    out = outs[0]
    for o in outs[1:]:
        out = out + o
    return out


def fox_mla_project(h, pos, w_in, b_f, g_q, g_kv, w_uq, w_uk):
    B, T, _ = h.shape
    proj = h @ w_in
    fq = proj[..., O_FQ:O_FK].reshape(B, T, HA, DA)
    fk = proj[..., O_FK:O_FV].reshape(B, T, HA, DA)
    fv = proj[..., O_FV:O_FG].reshape(B, T, HA, DA)
    logf = jax.nn.log_sigmoid((proj[..., O_FG:O_CQ] + b_f).astype(jnp.float32)).astype(h.dtype)
    cq = rmsnorm(proj[..., O_CQ:O_CKV], g_q)
    q = (cq @ w_uq).reshape(B, T, HB, NOPE_D + ROPE_D)
    q_pe = rope(q[..., NOPE_D:], pos, ROPE_D)
    q_lat = jnp.einsum('bthn,rhn->bthr', q[..., :NOPE_D], w_uk)
    ckv = rmsnorm(proj[..., O_CKV:O_KPE], g_kv)
    kpe = rope(proj[..., O_KPE:IN_A], pos, ROPE_D)
    return fq, fk, fv, logf, q_lat, q_pe, ckv, kpe


def fox_mla_output(o_fox, o_lat, w_uv, w_o):
    B, T = o_fox.shape[:2]
    o_mla = jnp.einsum('bthr,rhv->bthv', o_lat, w_uv)
    o = jnp.concatenate([o_fox.reshape(B, T, HA * DA), o_mla.reshape(B, T, HB * V_D)], axis=-1)
    return o @ w_o


def fox_mla_prompt(h, w_in, b_f, g_q, g_kv, w_uq, w_uk, w_uv, w_o):
    L = h.shape[1]
    pos = jnp.arange(L)
    fq, fk, fv, logf, q_lat, q_pe, ckv, kpe = fox_mla_project(h, pos, w_in, b_f, g_q, g_kv, w_uq, w_uk)
    cf = jnp.cumsum(logf.astype(jnp.float32), axis=1)
    lead = (-L) % BLOCK_Q
    nb = (L + lead) // BLOCK_Q
    qpos = (jnp.arange(nb * BLOCK_Q) - lead).reshape(nb, BLOCK_Q)

    def block(args):
        qf, cqf, ql, qp, qpos_b = args
        o_f = fox_attend(qf, cqf, qpos_b, ((fk, fv, cf, pos),))
        o_l = mla_attend(ql, qp, qpos_b, ((ckv, kpe, pos),))
        return o_f, o_l

    o_f, o_l = lax.map(block, (to_blocks(fq, lead), to_blocks(cf, lead), to_blocks(q_lat, lead),
                               to_blocks(q_pe, lead), qpos))
    out = fox_mla_output(from_blocks(o_f, lead), from_blocks(o_l, lead), w_uv, w_o)
    return out, fk, fv, logf, ckv, kpe


def fox_mla_sample(h, pos, pool_k, pool_v, pool_logf, pool_ckv, pool_kpe, page_table, i,
                   w_in, b_f, g_q, g_kv, w_uq, w_uk, w_uv, w_o):
    fq, fk, fv, logf, q_lat, q_pe, ckv, kpe = fox_mla_project(h, pos, w_in, b_f, g_q, g_kv, w_uq, w_uk)
    DB = h.shape[0]
    n_past = page_table.shape[1] * PAGE_SIZE

    def gather(pool):
        g = pool[i, page_table]
        return g.reshape((DB, n_past) + g.shape[3:])

    pk, pv, pl, pc, pr = gather(pool_k), gather(pool_v), gather(pool_logf), gather(pool_ckv), gather(pool_kpe)
    c_all = jnp.cumsum(jnp.concatenate([pl, logf], axis=1).astype(jnp.float32), axis=1)
    c_past, c_new = c_all[:, :n_past], c_all[:, n_past:]
    past_pos = jnp.arange(n_past)
    o_f = fox_attend(fq, c_new, pos, ((pk, pv, c_past, past_pos), (fk, fv, c_new, pos)))
    o_l = mla_attend(q_lat, q_pe, pos, ((pc, pr, past_pos), (ckv, kpe, pos)))
    return fox_mla_output(o_f, o_l, w_uv, w_o), fk, fv, logf, ckv, kpe


def swa_project(h, pos, w_qkv):
    B, T, _ = h.shape
    proj = h @ w_qkv
    q = rope(proj[..., :HC * HDC].reshape(B, T, HC, HDC), pos, ROT_C).reshape(B, T, KVC, GC, HDC)
    k = rope(proj[..., HC * HDC:(HC + KVC) * HDC].reshape(B, T, KVC, HDC), pos, ROT_C)
    v = proj[..., (HC + KVC) * HDC:].reshape(B, T, KVC, HDC)
    return q, k, v


def swa_attend(q, k, v, mask, sinks):
    s = jnp.einsum('bqngd,bsnd->bngqs', q, k).astype(jnp.float32) * SWA_SCALE
    s = jnp.where(mask, s, NEG_INF)
    sink = jnp.broadcast_to(sinks.astype(jnp.float32).reshape(KVC, GC, 1, 1), s.shape[:-1] + (1,))
    p = jax.nn.softmax(jnp.concatenate([s, sink], axis=-1), axis=-1)[..., :-1]
    return jnp.einsum('bngqs,bsnd->bqngd', p.astype(v.dtype), v)


def swa_prompt(h, w_qkv, sinks, w_o):
    B, L, _ = h.shape
    q, k, v = swa_project(h, jnp.arange(L), w_qkv)
    lead = (-L) % BLOCK_Q
    qb = to_blocks(q, lead)
    kb = to_blocks(k, lead + BLOCK_Q)
    vb = to_blocks(v, lead + BLOCK_Q)
    kband = jnp.concatenate([kb[:-1], kb[1:]], axis=2)
    vband = jnp.concatenate([vb[:-1], vb[1:]], axis=2)
    nb = qb.shape[0]
    qpos = (jnp.arange(nb * BLOCK_Q) - lead).reshape(nb, BLOCK_Q)
    kpos = (jnp.arange(nb)[:, None] - 1) * BLOCK_Q - lead + jnp.arange(2 * BLOCK_Q)[None, :]
    dq = qpos[:, :, None] - kpos[:, None, :]
    mask = (dq >= 0) & (dq < WINDOW) & (kpos[:, None, :] >= 0)
    ob = lax.map(lambda a: swa_attend(a[0], a[1], a[2], a[3], sinks), (qb, kband, vband, mask))
    o = from_blocks(ob, lead).reshape(B, L, HC * HDC)
    return o @ w_o, k[:, -WINDOW:], v[:, -WINDOW:]


def swa_sample(h, pos, buf_k, buf_v, w_qkv, sinks, w_o):
    DB, T, _ = h.shape
    q, k, v = swa_project(h, pos, w_qkv)
    kk = jnp.concatenate([buf_k, k], axis=1)
    vv = jnp.concatenate([buf_v, v], axis=1)
    nbuf = buf_k.shape[1]
    kpos = jnp.concatenate([PAST_LEN - nbuf + jnp.arange(nbuf), pos])
    dq = pos[:, None] - kpos[None, :]
    mask = (dq >= 0) & (dq < WINDOW)
    o = swa_attend(q, kk, vv, mask, sinks).reshape(DB, T, HC * HDC)
    return o @ w_o, kk[:, -WINDOW:], vv[:, -WINDOW:]


def setup_inputs(seed: int = 0) -> dict:
    key = jax.random.key(seed)
    ks = iter(jax.random.split(key, 40))

    def nrm(shape, scale):
        return jax.random.normal(next(ks), shape, jnp.float32) * scale

    n_pages = PAST_LEN // PAGE_SIZE
    n_used = DEC_BATCH * n_pages
    n_pool = n_used + (n_used + 3) // 4
    page_table = jax.random.permutation(next(ks), n_pool)[:n_used].reshape(DEC_BATCH, n_pages).astype(jnp.int32)
    d = D_MODEL
    a_w_in = jnp.concatenate([nrm((N_EVEN, d, 3 * HA * DA), d ** -0.5),
                              nrm((N_EVEN, d, HA), 0.5 * d ** -0.5),
                              nrm((N_EVEN, d, Q_RANK + KV_RANK + ROPE_D), d ** -0.5)], axis=-1)
    return {
        'x_prompt': nrm((BATCH, SEQ, d), 1.0),
        'x_sample': nrm((DEC_BATCH, DEC_SEQ, d), 1.0),
        'cache_fox_k': nrm((N_EVEN, n_pool, PAGE_SIZE, HA, DA), 1.0),
        'cache_fox_v': nrm((N_EVEN, n_pool, PAGE_SIZE, HA, DA), 1.0),
        'cache_fox_logf': jax.nn.log_sigmoid(nrm((N_EVEN, n_pool, PAGE_SIZE, HA), 1.0) + 4.0),
        'cache_mla_ckv': nrm((N_EVEN, n_pool, PAGE_SIZE, KV_RANK), 1.0),
        'cache_mla_kpe': nrm((N_EVEN, n_pool, PAGE_SIZE, ROPE_D), 1.0),
        'cache_swa_k': nrm((N_ODD, DEC_BATCH, WINDOW, KVC, HDC), 1.0),
        'cache_swa_v': nrm((N_ODD, DEC_BATCH, WINDOW, KVC, HDC), 1.0),
        'page_table': page_table,
        'meta_tokens': nrm((N_META, d), 1.0),
        'a_w_in': a_w_in,
        'a_b_f': jnp.linspace(1.0, 6.0, HA, dtype=jnp.float32)[None, :] + nrm((N_EVEN, HA), 0.1),
        'a_g_q': 1.0 + nrm((N_EVEN, Q_RANK), 0.02),
        'a_g_kv': 1.0 + nrm((N_EVEN, KV_RANK), 0.02),
        'a_w_uq': nrm((N_EVEN, Q_RANK, HB * (NOPE_D + ROPE_D)), Q_RANK ** -0.5),
        'a_w_uk': nrm((N_EVEN, KV_RANK, HB, NOPE_D), KV_RANK ** -0.5),
        'a_w_uv': nrm((N_EVEN, KV_RANK, HB, V_D), KV_RANK ** -0.5),
        'a_w_o': nrm((N_EVEN, HA * DA + HB * V_D, d), (HA * DA + HB * V_D) ** -0.5),
        'c_w_qkv': nrm((N_ODD, d, (HC + 2 * KVC) * HDC), d ** -0.5),
        'c_sinks': nrm((N_ODD, HC), 1.0),
        'c_w_o': nrm((N_ODD, HC * HDC, d), (HC * HDC) ** -0.5),
        'g_mix': 1.0 + nrm((DEPTH, d), 0.02),
        'g_ffn': 1.0 + nrm((DEPTH, d), 0.02),
        'w_gate': nrm((DEPTH, d, D_FF), d ** -0.5),
        'w_up': nrm((DEPTH, d, D_FF), d ** -0.5),
        'w_down': nrm((DEPTH, D_FF, d), D_FF ** -0.5),
        'g_final': 1.0 + nrm((d,), 0.02),
    }


def reference(x_prompt, x_sample, cache_fox_k, cache_fox_v, cache_fox_logf, cache_mla_ckv, cache_mla_kpe,
              cache_swa_k, cache_swa_v, page_table, meta_tokens, a_w_in, a_b_f, a_g_q, a_g_kv, a_w_uq, a_w_uk,
              a_w_uv, a_w_o, c_w_qkv, c_sinks, c_w_o, g_mix, g_ffn, w_gate, w_up, w_down, g_final):
    B = x_prompt.shape[0]
    meta = jnp.broadcast_to(meta_tokens.astype(x_prompt.dtype)[None], (B, N_META, D_MODEL))
    hp = jnp.concatenate([meta, x_prompt], axis=1)
    hs = x_sample
    pos_s = PAST_LEN + jnp.arange(x_sample.shape[1])
    fkp, fvp, flp, ckp, kpp, skp, svp = [], [], [], [], [], [], []
    fks, fvs, fls, cks, kps, sks, svs = [], [], [], [], [], [], []
    for layer in range(DEPTH):
        np_ = rmsnorm(hp, g_mix[layer])
        ns = rmsnorm(hs, g_mix[layer])
        if layer % 2 == 0:
            i = layer // 2
            w = (a_w_in[i], a_b_f[i], a_g_q[i], a_g_kv[i], a_w_uq[i], a_w_uk[i], a_w_uv[i], a_w_o[i])
            dp, k1, v1, l1, c1, r1 = fox_mla_prompt(np_, *w)
            ds, k2, v2, l2, c2, r2 = fox_mla_sample(ns, pos_s, cache_fox_k, cache_fox_v, cache_fox_logf,
                                                    cache_mla_ckv, cache_mla_kpe, page_table, i, *w)
            fkp.append(k1); fvp.append(v1); flp.append(l1); ckp.append(c1); kpp.append(r1)
            fks.append(k2); fvs.append(v2); fls.append(l2); cks.append(c2); kps.append(r2)
        else:
            j = layer // 2
            dp, k1, v1 = swa_prompt(np_, c_w_qkv[j], c_sinks[j], c_w_o[j])
            ds, k2, v2 = swa_sample(ns, pos_s, cache_swa_k[j], cache_swa_v[j], c_w_qkv[j], c_sinks[j], c_w_o[j])
            skp.append(k1); svp.append(v1); sks.append(k2); svs.append(v2)
        hp = hp + dp
        hs = hs + ds
        hp = hp + swiglu(rmsnorm(hp, g_ffn[layer]), w_gate[layer], w_up[layer], w_down[layer])
        hs = hs + swiglu(rmsnorm(hs, g_ffn[layer]), w_gate[layer], w_up[layer], w_down[layer])
    y_prompt = rmsnorm(hp, g_final)[:, N_META:]
    y_sample = rmsnorm(hs, g_final)
    fox_k_p, fox_v_p, fox_logf_p = jnp.stack(fkp), jnp.stack(fvp), jnp.stack(flp)
    mla_ckv_p, mla_kpe_p = jnp.stack(ckp), jnp.stack(kpp)
    swa_k_p, swa_v_p = jnp.stack(skp), jnp.stack(svp)
    fox_k_s, fox_v_s, fox_logf_s = jnp.stack(fks), jnp.stack(fvs), jnp.stack(fls)
    mla_ckv_s, mla_kpe_s = jnp.stack(cks), jnp.stack(kps)
    swa_k_s, swa_v_s = jnp.stack(sks), jnp.stack(svs)
    return (y_prompt, y_sample, fox_k_p, fox_v_p, fox_logf_p, mla_ckv_p, mla_kpe_p, swa_k_p, swa_v_p,
            fox_k_s, fox_v_s, fox_logf_s, mla_ckv_s, mla_kpe_s, swa_k_s, swa_v_s)
```

```python
import functools
import math

import jax
import jax.numpy as jnp
from jax import lax
from jax.experimental import pallas as pl
from jax.experimental.pallas import tpu as pltpu

F32 = jnp.float32
BF16 = jnp.bfloat16

N_META = 16
BLOCK = 128
ROPE_THETA = 500000.0
EPS = 1e-6
NEG = -1e30
DA = 128
NOPE_D = 128
ROPE_D = 64
V_D = 128
HDC = 64
ROT_C = HDC // 4
GQA = 8

LANES = 128
VMEM_LIMIT = 56 * 1024 * 1024


def _cparams(sem):
    return pltpu.CompilerParams(dimension_semantics=sem, vmem_limit_bytes=VMEM_LIMIT)


def _row_tile(m, cap, mult=16):
    best = None
    for t in range(mult, min(m, cap) + 1, mult):
        if m % t == 0:
            best = t
    assert best is not None, (m, cap)
    return best


def _col_tile(n, cap):
    if n <= cap:
        return n
    best = None
    for t in range(LANES, cap + 1, LANES):
        if n % t == 0:
            best = t
    assert best is not None, (n, cap)
    return best


def _rmsnorm_body(x_ref, g_ref, *o_refs):
    x = x_ref[...]
    ms = jnp.mean(x * x, axis=-1, keepdims=True)
    y = x * lax.rsqrt(ms + EPS) * g_ref[...]
    for o in o_refs:
        o[...] = y.astype(o.dtype)


def rmsnorm(x, g, out_dtypes, row_cap=512):
    m, d = x.shape
    tm = _row_tile(m, row_cap)
    outs = pl.pallas_call(
        _rmsnorm_body,
        grid=(m // tm,),
        in_specs=[pl.BlockSpec((tm, d), lambda i: (i, 0)), pl.BlockSpec((1, d), lambda i: (0, 0))],
        out_specs=[pl.BlockSpec((tm, d), lambda i: (i, 0)) for _ in out_dtypes],
        out_shape=[jax.ShapeDtypeStruct((m, d), dt) for dt in out_dtypes],
        compiler_params=_cparams(("parallel",)),
        name="rmsnorm",
    )(x, g.reshape(1, d).astype(F32))
    return outs


def _final_norm_body(x_ref, g_ref, o_ref):
    x = x_ref[0]
    ms = jnp.mean(x * x, axis=-1, keepdims=True)
    o_ref[0] = x * lax.rsqrt(ms + EPS) * g_ref[...]


def final_norm_prompt(hp, g, n_skip_blocks):
    b, lp, d = hp.shape
    nb = lp // BLOCK - n_skip_blocks
    return pl.pallas_call(
        _final_norm_body,
        grid=(b, nb),
        in_specs=[pl.BlockSpec((1, BLOCK, d), lambda i, j: (i, j + n_skip_blocks, 0)),
                  pl.BlockSpec((1, d), lambda i, j: (0, 0))],
        out_specs=pl.BlockSpec((1, BLOCK, d), lambda i, j: (i, j, 0)),
        out_shape=jax.ShapeDtypeStruct((b, nb * BLOCK, d), F32),
        compiler_params=_cparams(("parallel", "parallel")),
        name="final_norm_prompt",
    )(hp, g.reshape(1, d).astype(F32))


def _rope_groups(acc, c_ref, s1_ref, s2_ref, half):
    c, s1, s2 = c_ref[...], s1_ref[...], s2_ref[...]
    parts = []
    for j in range(acc.shape[1] // LANES):
        x = acc[:, j * LANES:(j + 1) * LANES]
        parts.append(x * c + pltpu.roll(x, LANES - half, 1) * s1 + pltpu.roll(x, half, 1) * s2)
    return parts[0] if len(parts) == 1 else jnp.concatenate(parts, axis=1)


def _mm_body(*refs, n_pairs, has_res, rope_half, scale):
    xs, ws = refs[:n_pairs], refs[n_pairs:2 * n_pairs]
    pos = 2 * n_pairs
    res_ref = None
    if has_res:
        res_ref = refs[pos]
        pos += 1
    if rope_half:
        c_ref, s1_ref, s2_ref = refs[pos:pos + 3]
        pos += 3
    o_refs = refs[pos:]
    acc = jnp.dot(xs[0][...], ws[0][...], preferred_element_type=F32)
    for x_ref, w_ref in zip(xs[1:], ws[1:]):
        acc = acc + jnp.dot(x_ref[...], w_ref[...], preferred_element_type=F32)
    if rope_half:
        acc = _rope_groups(acc, c_ref, s1_ref, s2_ref, rope_half)
    if scale != 1.0:
        acc = acc * scale
    if has_res:
        acc = acc + res_ref[...]
    for o in o_refs:
        o[...] = acc.astype(o.dtype)


def matmul(pairs, out_dtypes, res=None, rope=None, scale=1.0, tm_cap=896, tn_cap=512):
    m = pairs[0][0].shape[0]
    n = pairs[0][1].shape[1]
    tm = _row_tile(m, tm_cap)
    tn = _col_tile(n, tn_cap)
    in_specs, args = [], []
    for x, _ in pairs:
        in_specs.append(pl.BlockSpec((tm, x.shape[1]), lambda i, j: (i, 0)))
        args.append(x)
    for _, w in pairs:
        in_specs.append(pl.BlockSpec((w.shape[0], tn), lambda i, j: (0, j)))
        args.append(w)
    if res is not None:
        in_specs.append(pl.BlockSpec((tm, tn), lambda i, j: (i, j)))
        args.append(res)
    rope_half = 0
    if rope is not None:
        rope_half, tabs = rope
        for t in tabs:
            in_specs.append(pl.BlockSpec((tm, LANES), lambda i, j: (i, 0)))
            args.append(t)
    body = functools.partial(_mm_body, n_pairs=len(pairs), has_res=res is not None,
                             rope_half=rope_half, scale=scale)
    return pl.pallas_call(
        body,
        grid=(m // tm, n // tn),
        in_specs=in_specs,
        out_specs=[pl.BlockSpec((tm, tn), lambda i, j: (i, j)) for _ in out_dtypes],
        out_shape=[jax.ShapeDtypeStruct((m, n), dt) for dt in out_dtypes],
        compiler_params=_cparams(("parallel", "parallel")),
        name="matmul",
    )(*args)


def _swiglu_body(x_ref, wg_ref, wu_ref, o_ref):
    x = x_ref[...]
    g = jnp.dot(x, wg_ref[...], preferred_element_type=F32)
    u = jnp.dot(x, wu_ref[...], preferred_element_type=F32)
    o_ref[...] = (g * jax.nn.sigmoid(g) * u).astype(o_ref.dtype)


def swiglu_up(x, wg, wu, tm_cap=896, tn_cap=512):
    m, k = x.shape
    n = wg.shape[1]
    tm = _row_tile(m, tm_cap)
    tn = _col_tile(n, tn_cap)
    return pl.pallas_call(
        _swiglu_body,
        grid=(m // tm, n // tn),
        in_specs=[pl.BlockSpec((tm, k), lambda i, j: (i, 0)),
                  pl.BlockSpec((k, tn), lambda i, j: (0, j)),
                  pl.BlockSpec((k, tn), lambda i, j: (0, j))],
        out_specs=pl.BlockSpec((tm, tn), lambda i, j: (i, j)),
        out_shape=jax.ShapeDtypeStruct((m, n), BF16),
        compiler_params=_cparams(("parallel", "parallel")),
        name="swiglu_up",
    )(x, wg, wu)


def _head_mm_body(x_ref, w_ref, o_ref):
    o_ref[...] = jnp.dot(x_ref[...], w_ref[0], preferred_element_type=F32).astype(o_ref.dtype)


def head_matmul(x, w, out_dtype):
    m = x.shape[0]
    nh, k, n = w.shape
    return pl.pallas_call(
        _head_mm_body,
        grid=(nh,),
        in_specs=[pl.BlockSpec((m, k), lambda h: (0, h)), pl.BlockSpec((1, k, n), lambda h: (h, 0, 0))],
        out_specs=pl.BlockSpec((m, n), lambda h: (0, h)),
        out_shape=jax.ShapeDtypeStruct((m, nh * n), out_dtype),
        compiler_params=_cparams(("parallel",)),
        name="head_matmul",
    )(x, w)


def _split3(x):
    hi = x.astype(BF16)
    r1 = x - hi.astype(F32)
    mid = r1.astype(BF16)
    lo = (r1 - mid.astype(F32)).astype(BF16)
    return hi, mid, lo


def _gate_body(z_ref, b_ref, logf_ref, cfc_ref, cfr_ref, carry_ref, *, blocks_per_seq, n_prompt_blocks,
               group):
    i = pl.program_id(0)
    is_sample = i >= n_prompt_blocks

    @pl.when(jnp.logical_or(i % blocks_per_seq == 0, is_sample))
    def _():
        carry_ref[...] = jnp.zeros_like(carry_ref)

    x = z_ref[...] + b_ref[...]
    lf = jnp.minimum(x, 0.0) - jnp.log1p(jnp.exp(-jnp.abs(x)))
    logf_ref[...] = lf
    row = lax.broadcasted_iota(jnp.int32, (BLOCK, BLOCK), 0)
    col = lax.broadcasted_iota(jnp.int32, (BLOCK, BLOCK), 1)
    same_group = jnp.logical_or(jnp.logical_not(is_sample), row // group == col // group)
    tri = jnp.where(jnp.logical_and(col <= row, same_group), 1.0, 0.0).astype(BF16)
    hi, mid, lo = _split3(lf)
    cf = (jnp.dot(tri, hi, preferred_element_type=F32) + jnp.dot(tri, mid, preferred_element_type=F32)
          + jnp.dot(tri, lo, preferred_element_type=F32)) + carry_ref[...]
    carry_ref[...] = cf[BLOCK - 1:BLOCK, :]
    cfc_ref[...] = cf
    cfr_ref[...] = cf.T[:8, :]


def forget_gates(z, b_pad, blocks_per_seq, n_prompt_blocks, group):
    t = z.shape[0]
    body = functools.partial(_gate_body, blocks_per_seq=blocks_per_seq, n_prompt_blocks=n_prompt_blocks,
                             group=group)
    return pl.pallas_call(
        body,
        grid=(t // BLOCK,),
        in_specs=[pl.BlockSpec((BLOCK, LANES), lambda i: (i, 0)), pl.BlockSpec((1, LANES), lambda i: (0, 0))],
        out_specs=[pl.BlockSpec((BLOCK, LANES), lambda i: (i, 0)),
                   pl.BlockSpec((BLOCK, LANES), lambda i: (i, 0)),
                   pl.BlockSpec((8, BLOCK), lambda i: (0, i))],
        out_shape=[jax.ShapeDtypeStruct((t, LANES), F32), jax.ShapeDtypeStruct((t, LANES), F32),
                   jax.ShapeDtypeStruct((8, t), F32)],
        scratch_shapes=[pltpu.VMEM((1, LANES), F32)],
        compiler_params=_cparams(("arbitrary",)),
        name="forget_gates",
    )(z, b_pad)


def _flash_body(*refs, tq, lead, gated, two_part):
    if two_part:
        q_ref, q2_ref, k_ref, k2_ref, v_ref, o_ref, kcat_ref, m_ref, l_ref, acc_ref = refs
    elif gated:
        q_ref, k_ref, v_ref, cfc_ref, cfr_ref, o_ref, m_ref, l_ref, acc_ref = refs
    else:
        q_ref, k_ref, v_ref, o_ref, m_ref, l_ref, acc_ref = refs
    h = pl.program_id(1)
    qi = pl.program_id(2)

    if two_part:
        @pl.when(qi == 0)
        def _():
            kcat_ref[:, :LANES] = k_ref[0]
            kcat_ref[:, LANES:] = k2_ref[0]
        q = jnp.concatenate([q_ref[0], q2_ref[0]], axis=1)
    else:
        q = q_ref[0]

    if gated:
        lane = lax.broadcasted_iota(jnp.int32, (tq, LANES), 1)
        cq = jnp.sum(jnp.where(lane == h, cfc_ref[0], 0.0), axis=1, keepdims=True)

    m_ref[...] = jnp.full_like(m_ref, NEG)
    l_ref[...] = jnp.zeros_like(l_ref)
    acc_ref[...] = jnp.zeros_like(acc_ref)
    qidx = qi * tq + lax.broadcasted_iota(jnp.int32, (tq, tq), 0)
    kofs = lax.broadcasted_iota(jnp.int32, (tq, tq), 1)

    def step(kj, _):
        off = pl.multiple_of(kj * tq, tq)
        if two_part:
            k = kcat_ref[pl.ds(off, tq), :]
        else:
            k = k_ref[0, pl.ds(off, tq), :]
        s = lax.dot_general(q, k, (((1,), (1,)), ((), ())), preferred_element_type=F32)
        if gated:
            s = s + (cq - cfr_ref[0, pl.ds(h, 1), pl.ds(off, tq)])
        kidx = off + kofs
        s = jnp.where(jnp.logical_and(kidx <= qidx, kidx >= lead), s, NEG)
        m_old = m_ref[...]
        m_new = jnp.maximum(m_old, jnp.max(s, axis=1, keepdims=True))
        alpha = jnp.exp(m_old - m_new)
        p = jnp.exp(s - m_new)
        l_ref[...] = alpha * l_ref[...] + jnp.sum(p, axis=1, keepdims=True)
        acc_ref[...] = alpha * acc_ref[...] + jnp.dot(p.astype(BF16), v_ref[0, pl.ds(off, tq), :],
                                                      preferred_element_type=F32)
        m_ref[...] = m_new
        return 0

    lax.fori_loop(0, qi + 1, step, 0)
    o_ref[0] = (acc_ref[...] / l_ref[...]).astype(o_ref.dtype)


def flash_prompt(q, k, v, lead, tq, cf_col=None, cf_row=None, q2=None, k2=None):
    b, lp, hd = q.shape
    nh = hd // LANES
    gated = cf_col is not None
    two_part = q2 is not None
    qspec = pl.BlockSpec((1, tq, LANES), lambda bi, h, qi: (bi, qi, h))
    kvspec = pl.BlockSpec((1, lp, LANES), lambda bi, h, qi: (bi, 0, h))
    in_specs, args = [qspec], [q]
    if two_part:
        in_specs.append(qspec)
        args.append(q2)
    in_specs.append(kvspec)
    args.append(k)
    if two_part:
        in_specs.append(pl.BlockSpec((1, lp, LANES), lambda bi, h, qi: (bi, 0, 0)))
        args.append(k2)
    in_specs.append(kvspec)
    args.append(v)
    if gated:
        in_specs += [pl.BlockSpec((1, tq, LANES), lambda bi, h, qi: (bi, qi, 0)),
                     pl.BlockSpec((1, 8, lp), lambda bi, h, qi: (bi, 0, 0))]
        args += [cf_col, cf_row]
    scratch = []
    if two_part:
        scratch.append(pltpu.VMEM((lp, 2 * LANES), BF16))
    scratch += [pltpu.VMEM((tq, 1), F32), pltpu.VMEM((tq, 1), F32), pltpu.VMEM((tq, LANES), F32)]
    body = functools.partial(_flash_body, tq=tq, lead=lead, gated=gated, two_part=two_part)
    return pl.pallas_call(
        body,
        grid=(b, nh, lp // tq),
        in_specs=in_specs,
        out_specs=pl.BlockSpec((1, tq, LANES), lambda bi, h, qi: (bi, qi, h)),
        out_shape=jax.ShapeDtypeStruct((b, lp, hd), BF16),
        scratch_shapes=scratch,
        compiler_params=_cparams(("parallel", "parallel", "arbitrary")),
        name="flash_prompt",
    )(*args)


def _lane_sum(x):
    return jnp.sum(x, axis=1, keepdims=True)


def _decode_body(pt_ref, qbd_ref, qlat_ref, qpe_ref, cnew_ref, knew_ref, vnew_ref, cnew_kv_ref, penew_ref,
                 *refs, pages_per_step, n_new, n_heads):
    pp = pages_per_step
    k_refs = refs[0:pp]
    v_refs = refs[pp:2 * pp]
    lf_refs = refs[2 * pp:3 * pp]
    c_refs = refs[3 * pp:4 * pp]
    pe_refs = refs[4 * pp:5 * pp]
    ofox_ref, olat_ref = refs[5 * pp:5 * pp + 2]
    mf_ref, lfs_ref, accf_ref, ml_ref, ll_ref, accl_ref, carry_ref = refs[5 * pp + 2:]
    j = pl.program_id(1)
    nrows = n_new * n_heads
    rows = lax.broadcasted_iota(jnp.int32, (nrows, LANES), 0)
    row_t = rows // n_heads
    row_h = rows % n_heads

    @pl.when(j == 0)
    def _():
        qbd = qbd_ref[0].astype(F32)
        qlat = qlat_ref[0].astype(F32)
        qpe = qpe_ref[0].astype(F32)
        cnew = cnew_ref[0]
        sf, sl = [], []
        for t in range(n_new):
            visible = row_t >= t
            ck = jnp.concatenate([cnew[t * n_heads:(t + 1) * n_heads]] * n_new, axis=0)
            s = _lane_sum(qbd * knew_ref[0, t:t + 1, :]) + (cnew - ck)
            sf.append(jnp.where(visible, s, NEG))
            s = _lane_sum(qlat * cnew_kv_ref[0, t:t + 1, :]) + _lane_sum(qpe * penew_ref[0, t:t + 1, :])
            sl.append(jnp.where(visible, jnp.broadcast_to(s, (nrows, LANES)), NEG))
        for ss, m_ref, l_ref, acc_ref, val_ref in ((sf, mf_ref, lfs_ref, accf_ref, vnew_ref),
                                                   (sl, ml_ref, ll_ref, accl_ref, cnew_kv_ref)):
            m = functools.reduce(jnp.maximum, ss)
            ps = [jnp.exp(s - m) for s in ss]
            m_ref[...] = m
            l_ref[...] = functools.reduce(jnp.add, ps)
            acc = ps[0][:, :1] * val_ref[0, 0:1, :]
            for t in range(1, n_new):
                acc = acc + ps[t][:, :1] * val_ref[0, t:t + 1, :]
            acc_ref[...] = acc
        carry_ref[...] = jnp.zeros_like(carry_ref)

    kk = lax.broadcasted_iota(jnp.int32, (BLOCK, BLOCK), 0)
    jj = lax.broadcasted_iota(jnp.int32, (BLOCK, BLOCK), 1)
    later = jnp.where(kk > jj, 1.0, 0.0).astype(BF16)
    nt = (((1,), (1,)), ((), ()))

    def online(s, m_ref, l_ref, acc_ref, vals):
        m_old = m_ref[...]
        m_new = jnp.maximum(m_old, jnp.max(s, axis=1, keepdims=True))
        alpha = jnp.exp(m_old - m_new)
        p = jnp.exp(s - m_new)
        l_ref[...] = alpha * l_ref[...] + _lane_sum(p)
        acc_ref[...] = alpha[:, :1] * acc_ref[...] + jnp.dot(p.astype(BF16), vals, preferred_element_type=F32)
        m_ref[...] = m_new

    for r in range(pp):
        lf = lf_refs[r][0]
        hi, mid, lo = _split3(lf)
        stacked = jnp.concatenate([hi.astype(F32), mid.astype(F32), lo.astype(F32),
                                   jnp.zeros_like(lf)], axis=0).astype(BF16)
        part = jnp.dot(stacked, later, preferred_element_type=F32)
        nhp = lf.shape[0]
        after = part[0:nhp] + part[nhp:2 * nhp] + part[2 * nhp:3 * nhp] + carry_ref[...]
        carry_ref[...] = carry_ref[...] + _lane_sum(lf)
        bias = jnp.concatenate([after] * n_new, axis=0) + cnew_ref[0]
        kb = k_refs[r][0].astype(BF16)
        s = lax.dot_general(qbd_ref[0], kb, nt, preferred_element_type=F32) + bias
        online(s, mf_ref, lfs_ref, accf_ref, v_refs[r][0].astype(BF16))

        cb = c_refs[r][0].astype(BF16)
        peb = pe_refs[r][0].astype(BF16)
        s = (lax.dot_general(qlat_ref[0], cb, nt, preferred_element_type=F32)
             + lax.dot_general(qpe_ref[0][:, :peb.shape[1]], peb, nt, preferred_element_type=F32))
        online(s, ml_ref, ll_ref, accl_ref, cb)

    @pl.when(j == pl.num_programs(1) - 1)
    def _():
        acc = accf_ref[...]
        o = jnp.zeros((nrows, DA), F32)
        for hh in range(n_heads):
            o = o + jnp.where(row_h == hh, acc[:, hh * DA:(hh + 1) * DA], 0.0)
        ofox_ref[0] = (o / lfs_ref[...]).astype(ofox_ref.dtype)
        olat_ref[0] = (accl_ref[...] / ll_ref[:, :1]).astype(olat_ref.dtype)


def decode_attention(page_table, qbd, qlat, qpe, cnew, knew, vnew, ckv_new, kpe_new,
                     pool_k, pool_v, pool_lft, pool_ckv, pool_kpe, pages_per_step):
    db, n_pages = page_table.shape
    pp = pages_per_step
    assert n_pages % pp == 0
    nrows = qbd.shape[1]
    n_new = knew.shape[1]
    n_heads = nrows // n_new
    kv_rank = pool_ckv.shape[2]

    def seq_spec(a):
        return pl.BlockSpec((1,) + a.shape[1:], lambda b, j, pt: (b,) + (0,) * (a.ndim - 1))

    def page_spec(a, r):
        def imap(b, j, pt):
            return (pt[b, n_pages - 1 - (j * pp + r)],) + (0,) * (a.ndim - 1)
        return pl.BlockSpec((1,) + a.shape[1:], imap)

    seq_args = [qbd, qlat, qpe, cnew, knew, vnew, ckv_new, kpe_new]
    in_specs = [seq_spec(a) for a in seq_args]
    args = list(seq_args)
    for pool in (pool_k, pool_v, pool_lft, pool_ckv, pool_kpe):
        for r in range(pp):
            in_specs.append(page_spec(pool, r))
            args.append(pool)
    body = functools.partial(_decode_body, pages_per_step=pp, n_new=n_new, n_heads=n_heads)
    grid_spec = pltpu.PrefetchScalarGridSpec(
        num_scalar_prefetch=1,
        grid=(db, n_pages // pp),
        in_specs=in_specs,
        out_specs=[pl.BlockSpec((1, nrows, DA), lambda b, j, pt: (b, 0, 0)),
                   pl.BlockSpec((1, nrows, kv_rank), lambda b, j, pt: (b, 0, 0))],
        scratch_shapes=[pltpu.VMEM((nrows, LANES), F32), pltpu.VMEM((nrows, LANES), F32),
                        pltpu.VMEM((nrows, pool_k.shape[2]), F32),
                        pltpu.VMEM((nrows, LANES), F32), pltpu.VMEM((nrows, LANES), F32),
                        pltpu.VMEM((nrows, kv_rank), F32),
                        pltpu.VMEM((n_heads, LANES), F32)],
    )
    return pl.pallas_call(
        body,
        grid_spec=grid_spec,
        out_shape=[jax.ShapeDtypeStruct((db, nrows, DA), BF16),
                   jax.ShapeDtypeStruct((db, nrows, kv_rank), BF16)],
        compiler_params=_cparams(("parallel", "arbitrary")),
        name="decode_attention",
    )(page_table, *args)


def _split_pairs(x):
    lane = lax.broadcasted_iota(jnp.int32, x.shape, 1)
    lo = jnp.where(lane < HDC, x, 0.0)
    hi = jnp.where(lane >= HDC, x, 0.0)
    return (lo.astype(BF16), pltpu.roll(lo, HDC, 1).astype(BF16),
            pltpu.roll(hi, HDC, 1).astype(BF16), hi.astype(BF16))


def _swa_prompt_body(sink_ref, q_ref, kp_ref, kc_ref, vp_ref, vc_ref, o_ref, *, lead, n_kv):
    i = pl.program_id(1)
    kk = jnp.concatenate([kp_ref[0], kc_ref[0]], axis=0).astype(F32)
    vv = jnp.concatenate([vp_ref[0], vc_ref[0]], axis=0).astype(F32)
    k_lo, k_hi, v_lo, v_hi = [], [], [], []
    for c in range(n_kv // 2):
        a_lo, a_hi, b_lo, b_hi = _split_pairs(kk[:, c * LANES:(c + 1) * LANES])
        k_lo += [a_lo, b_lo]
        k_hi += [a_hi, b_hi]
        a_lo, a_hi, b_lo, b_hi = _split_pairs(vv[:, c * LANES:(c + 1) * LANES])
        v_lo += [a_lo, b_lo]
        v_hi += [a_hi, b_hi]
    row = lax.broadcasted_iota(jnp.int32, (BLOCK, 2 * BLOCK), 0)
    col = lax.broadcasted_iota(jnp.int32, (BLOCK, 2 * BLOCK), 1)
    kidx = (i - 1) * BLOCK + col
    dq = BLOCK + row - col
    mask = jnp.logical_and(jnp.logical_and(dq >= 0, dq < BLOCK), kidx >= lead)
    lane = lax.broadcasted_iota(jnp.int32, (BLOCK, LANES), 1)
    nt = (((1,), (1,)), ((), ()))
    for pair in range(n_kv * GQA // 2):
        qp = q_ref[0, :, pair * LANES:(pair + 1) * LANES]
        o = jnp.zeros((BLOCK, LANES), F32)
        for odd in range(2):
            head = 2 * pair + odd
            n = head // GQA
            qsel = jnp.where((lane >= HDC) if odd else (lane < HDC), qp, jnp.zeros_like(qp))
            s = lax.dot_general(qsel, (k_hi if odd else k_lo)[n], nt, preferred_element_type=F32)
            s = jnp.where(mask, s, NEG)
            sink = sink_ref[head]
            m = jnp.maximum(jnp.max(s, axis=1, keepdims=True), sink)
            p = jnp.exp(s - m)
            denom = jnp.sum(p, axis=1, keepdims=True) + jnp.exp(sink - m)
            p = (p / denom).astype(BF16)
            o = o + jnp.dot(p, (v_hi if odd else v_lo)[n], preferred_element_type=F32)
        o_ref[0, :, pair * LANES:(pair + 1) * LANES] = o.astype(o_ref.dtype)


def swa_prompt(q, k, v, sinks, lead):
    b, lp, dq = q.shape
    dkv = k.shape[2]
    n_kv = dkv // HDC
    cur = lambda bi, i, s: (bi, i, 0)
    prev = lambda bi, i, s: (bi, jnp.maximum(i - 1, 0), 0)
    grid_spec = pltpu.PrefetchScalarGridSpec(
        num_scalar_prefetch=1,
        grid=(b, lp // BLOCK),
        in_specs=[pl.BlockSpec((1, BLOCK, dq), cur),
                  pl.BlockSpec((1, BLOCK, dkv), prev), pl.BlockSpec((1, BLOCK, dkv), cur),
                  pl.BlockSpec((1, BLOCK, dkv), prev), pl.BlockSpec((1, BLOCK, dkv), cur)],
        out_specs=pl.BlockSpec((1, BLOCK, dq), cur),
    )
    return pl.pallas_call(
        functools.partial(_swa_prompt_body, lead=lead, n_kv=n_kv),
        grid_spec=grid_spec,
        out_shape=jax.ShapeDtypeStruct((b, lp, dq), BF16),
        compiler_params=_cparams(("parallel", "parallel")),
        name="swa_prompt",
    )(sinks, q, k, k, v, v)


def _swa_sample_body(q_ref, sink_ref, kbuf_ref, vbuf_ref, knew_ref, vnew_ref, o_ref, kout_ref, vout_ref,
                     *, n_new, n_kv):
    win = kbuf_ref.shape[1]
    nrows = q_ref.shape[2]
    kout_ref[0, 0:win - n_new, :] = kbuf_ref[0, n_new:win, :]
    kout_ref[0, win - n_new:win, :] = knew_ref[0]
    vout_ref[0, 0:win - n_new, :] = vbuf_ref[0, n_new:win, :]
    vout_ref[0, win - n_new:win, :] = vnew_ref[0]
    kb, vb = kbuf_ref[0], vbuf_ref[0]
    kn, vn = knew_ref[0], vnew_ref[0]
    k_lo, v_lo, kn_lo, vn_lo = [], [], [], []
    for c in range(n_kv // 2):
        sl = slice(c * LANES, (c + 1) * LANES)
        a_lo, _, b_lo, _ = _split_pairs(kb[:, sl])
        k_lo += [a_lo, b_lo]
        a_lo, _, b_lo, _ = _split_pairs(vb[:, sl])
        v_lo += [a_lo, b_lo]
        lane = lax.broadcasted_iota(jnp.int32, (n_new, LANES), 1)
        for src, dst in ((kn, kn_lo), (vn, vn_lo)):
            x = src[:, sl]
            dst += [jnp.where(lane < HDC, x, 0.0), pltpu.roll(jnp.where(lane >= HDC, x, 0.0), HDC, 1)]
    row_t = lax.broadcasted_iota(jnp.int32, (nrows, win), 0) % n_new
    key = lax.broadcasted_iota(jnp.int32, (nrows, win), 1)
    buf_visible = key > row_t
    row_t1 = lax.broadcasted_iota(jnp.int32, (nrows, 1), 0) % n_new
    nt = (((1,), (1,)), ((), ()))
    for n in range(n_kv):
        q = q_ref[0, n]
        qf = q.astype(F32)
        sink = sink_ref[n]
        s = lax.dot_general(q, k_lo[n], nt, preferred_element_type=F32)
        s = jnp.where(buf_visible, s, NEG)
        s_new = [jnp.where(row_t1 >= t, _lane_sum(qf * kn_lo[n][t:t + 1, :]), NEG) for t in range(n_new)]
        m = jnp.maximum(jnp.max(s, axis=1, keepdims=True), sink[:, :1])
        for sn in s_new:
            m = jnp.maximum(m, sn)
        p = jnp.exp(s - m)
        p_new = [jnp.exp(sn - m) for sn in s_new]
        denom = _lane_sum(p) + jnp.exp(sink[:, :1] - m)
        for pn in p_new:
            denom = denom + pn
        o = jnp.dot((p / denom).astype(BF16), v_lo[n], preferred_element_type=F32)
        for t in range(n_new):
            o = o + (p_new[t] / denom) * vn_lo[n][t:t + 1, :]
        o_ref[0, n] = o.astype(o_ref.dtype)


def swa_sample(q_rows, sink_rows, kbuf, vbuf, knew, vnew):
    db, n_kv, nrows, _ = q_rows.shape
    win, dkv = kbuf.shape[1:]
    n_new = knew.shape[1]
    seq3 = lambda b: (b, 0, 0)
    return pl.pallas_call(
        functools.partial(_swa_sample_body, n_new=n_new, n_kv=n_kv),
        grid=(db,),
        in_specs=[pl.BlockSpec((1, n_kv, nrows, LANES), lambda b: (b, 0, 0, 0)),
                  pl.BlockSpec((n_kv, nrows, LANES), lambda b: (0, 0, 0)),
                  pl.BlockSpec((1, win, dkv), seq3), pl.BlockSpec((1, win, dkv), seq3),
                  pl.BlockSpec((1, n_new, dkv), seq3), pl.BlockSpec((1, n_new, dkv), seq3)],
        out_specs=[pl.BlockSpec((1, n_kv, nrows, LANES), lambda b: (b, 0, 0, 0)),
                   pl.BlockSpec((1, win, dkv), seq3), pl.BlockSpec((1, win, dkv), seq3)],
        out_shape=[jax.ShapeDtypeStruct((db, n_kv, nrows, LANES), BF16),
                   jax.ShapeDtypeStruct((db, win, dkv), F32), jax.ShapeDtypeStruct((db, win, dkv), F32)],
        compiler_params=_cparams(("parallel",)),
        name="swa_sample",
    )(q_rows, sink_rows, kbuf, vbuf, knew, vnew)


def _rope_tables(pos, rot, period):
    half = rot // 2
    inv_freq = jnp.exp(jnp.arange(half, dtype=F32) * (-2.0 * math.log(ROPE_THETA) / rot))
    ang = pos.astype(F32)[:, None] * inv_freq
    cos, sin = jnp.cos(ang), jnp.sin(ang)
    t = pos.shape[0]
    pad = jnp.zeros((t, period - rot), F32)
    c = jnp.concatenate([cos, cos, pad + 1.0], axis=1)
    s1 = jnp.concatenate([-sin, jnp.zeros_like(sin), pad], axis=1)
    s2 = jnp.concatenate([jnp.zeros_like(sin), sin, pad], axis=1)
    rep = LANES // period
    return half, tuple(jnp.tile(a, (1, rep)) for a in (c, s1, s2))


def _ffn(h, g, wg, wu, wd):
    (n,) = rmsnorm(h, g, [BF16])
    a = swiglu_up(n, wg, wu)
    (out,) = matmul([(a, wd)], [F32], res=h, tm_cap=640)
    return out


def kernel(x_prompt, x_sample, cache_fox_k, cache_fox_v, cache_fox_logf, cache_mla_ckv, cache_mla_kpe,
           cache_swa_k, cache_swa_v, page_table, meta_tokens, a_w_in, a_b_f, a_g_q, a_g_kv, a_w_uq, a_w_uk,
           a_w_uv, a_w_o, c_w_qkv, c_sinks, c_w_o, g_mix, g_ffn, w_gate, w_up, w_down, g_final):
    B, SEQ, D = x_prompt.shape
    DB, DS, _ = x_sample.shape
    n_pool, PAGE = cache_fox_k.shape[1:3]
    HA = cache_fox_k.shape[3]
    KV_RANK = cache_mla_ckv.shape[3]
    Q_RANK = a_g_q.shape[1]
    HB = a_w_uk.shape[2]
    n_pages = page_table.shape[1]
    PAST = n_pages * PAGE
    WINDOW, KVC = cache_swa_k.shape[2:4]
    HC = KVC * GQA
    assert PAGE == BLOCK and WINDOW == BLOCK and DA == LANES and HA <= 8
    L = SEQ + N_META
    LEAD = (-L) % BLOCK
    LP = L + LEAD
    TP = B * LP
    TS = DB * DS
    T = TP + TS
    fox_scale = 1.0 / math.sqrt(DA)
    mla_scale = 1.0 / math.sqrt(NOPE_D + ROPE_D)
    swa_scale = 1.0 / math.sqrt(HDC)

    meta = jnp.broadcast_to(meta_tokens.astype(F32)[None], (B, N_META, D))
    hp = jnp.concatenate([jnp.zeros((B, LEAD, D), F32), meta, x_prompt], axis=1)
    h = jnp.concatenate([hp.reshape(TP, D), x_sample.reshape(TS, D)], axis=0)
    pos = jnp.concatenate([jnp.tile(jnp.arange(LP) - LEAD, B), jnp.tile(PAST + jnp.arange(DS), DB)])
    half_mla, tabs_mla = _rope_tables(pos, ROPE_D, LANES)
    half_swa, tabs_swa = _rope_tables(pos, ROT_C, HDC)

    w_in = a_w_in[0]
    o_fk, o_fv, o_fg = HA * DA, 2 * HA * DA, 3 * HA * DA
    o_cq = o_fg + HA
    o_ckv = o_cq + Q_RANK
    o_kpe = o_ckv + KV_RANK
    w_fq = w_in[:, :o_fk].astype(BF16)
    w_fk = w_in[:, o_fk:o_fv].astype(BF16)
    w_fv = w_in[:, o_fv:o_fg].astype(BF16)
    w_fg = jnp.pad(w_in[:, o_fg:o_cq], ((0, 0), (0, LANES - HA))).astype(BF16)
    w_cq = w_in[:, o_cq:o_ckv].astype(BF16)
    w_ckv = w_in[:, o_ckv:o_kpe].astype(BF16)
    w_kpe = jnp.pad(w_in[:, o_kpe:], ((0, 0), (0, LANES - ROPE_D))).astype(BF16)
    w_uq = a_w_uq[0].reshape(Q_RANK, HB, NOPE_D + ROPE_D)
    w_uq_nope = w_uq[:, :, :NOPE_D].reshape(Q_RANK, HB * NOPE_D).astype(BF16)
    w_uq_pe = jnp.pad(w_uq[:, :, NOPE_D:], ((0, 0), (0, 0), (0, LANES - ROPE_D))).reshape(Q_RANK, HB * LANES)
    w_uq_pe = w_uq_pe.astype(BF16)
    w_uk = a_w_uk[0]
    w_uv = a_w_uv[0]
    w_uk_flat = w_uk.reshape(KV_RANK, HB * NOPE_D).astype(BF16)
    w_uv_flat = w_uv.reshape(KV_RANK, HB * V_D).astype(BF16)
    w_uk_heads = jnp.transpose(w_uk, (1, 2, 0)).astype(BF16)
    w_uv_heads = jnp.transpose(w_uv, (1, 0, 2)).astype(BF16)
    w_o = a_w_o[0].astype(BF16)
    b_pad = jnp.pad(a_b_f[0], (0, LANES - HA)).reshape(1, LANES)

    (n0,) = rmsnorm(h, g_mix[0], [BF16])
    (fq,) = matmul([(n0, w_fq)], [BF16], scale=fox_scale)
    fk32, fk = matmul([(n0, w_fk)], [F32, BF16])
    fv32, fv = matmul([(n0, w_fv)], [F32, BF16])
    (z,) = matmul([(n0, w_fg)], [F32])
    (cq_raw,) = matmul([(n0, w_cq)], [F32])
    (ckv_raw,) = matmul([(n0, w_ckv)], [F32])
    kpe32, kpe = matmul([(n0, w_kpe)], [F32, BF16], rope=(half_mla, tabs_mla))
    (cq,) = rmsnorm(cq_raw, a_g_q[0], [BF16])
    ckv32, ckv = rmsnorm(ckv_raw, a_g_kv[0], [F32, BF16])
    (q_nope,) = matmul([(cq, w_uq_nope)], [BF16], scale=mla_scale)
    (q_pe,) = matmul([(cq, w_uq_pe)], [BF16], rope=(half_mla, tabs_mla), scale=mla_scale)
    logf, cf_col, cf_row = forget_gates(z, b_pad, LP // BLOCK, TP // BLOCK, DS)

    tq = _row_tile(LP, 512, BLOCK)
    p3 = lambda a: a[:TP].reshape(B, LP, a.shape[1])
    cf_row_p = jnp.transpose(cf_row[:, :TP].reshape(8, B, LP), (1, 0, 2))
    o_fox_p = flash_prompt(p3(fq), p3(fk), p3(fv), LEAD, tq, cf_col=p3(cf_col), cf_row=cf_row_p)
    (k_nope,) = matmul([(ckv[:TP], w_uk_flat)], [BF16])
    (v_mla,) = matmul([(ckv[:TP], w_uv_flat)], [BF16])
    o_mla_p = flash_prompt(p3(q_nope), p3(k_nope), v_mla.reshape(B, LP, HB * V_D), LEAD, tq,
                           q2=p3(q_pe), k2=p3(kpe))

    nrows = DS * HA
    q_lat = head_matmul(q_nope[TP:], w_uk_heads, BF16)
    eye = jnp.eye(HA, dtype=BF16)
    qbd = (fq[TP:].reshape(DB, DS, HA, 1, DA) * eye[None, None, :, :, None]).reshape(DB, nrows, HA * DA)
    cnew = jnp.broadcast_to(cf_col[TP:, :HA].reshape(DB, nrows, 1), (DB, nrows, LANES))
    lft_pool = jnp.transpose(cache_fox_logf[0], (0, 2, 1))
    pages_per_step = 4 if n_pages % 4 == 0 else 1
    o_fox_s, o_lat_s = decode_attention(
        page_table, qbd, q_lat.reshape(DB, nrows, KV_RANK), q_pe[TP:].reshape(DB, nrows, LANES), cnew,
        fk32[TP:].reshape(DB, DS, HA * DA), fv32[TP:].reshape(DB, DS, HA * DA),
        ckv32[TP:].reshape(DB, DS, KV_RANK), kpe32[TP:].reshape(DB, DS, LANES),
        cache_fox_k.reshape(n_pool, PAGE, HA * DA), cache_fox_v.reshape(n_pool, PAGE, HA * DA), lft_pool,
        cache_mla_ckv.reshape(n_pool, PAGE, KV_RANK), cache_mla_kpe.reshape(n_pool, PAGE, ROPE_D),
        pages_per_step)
    o_mla_s = head_matmul(o_lat_s.reshape(TS, HB * KV_RANK), w_uv_heads, BF16)

    o_fox = jnp.concatenate([o_fox_p.reshape(TP, HA * DA), o_fox_s.reshape(TS, HA * DA)], axis=0)
    o_mla = jnp.concatenate([o_mla_p.reshape(TP, HB * V_D), o_mla_s], axis=0)
    (h,) = matmul([(o_fox, w_o[:HA * DA]), (o_mla, w_o[HA * DA:])], [F32], res=h)
    h = _ffn(h, g_ffn[0], w_gate[0].astype(BF16), w_up[0].astype(BF16), w_down[0].astype(BF16))

    w_qkv = c_w_qkv[0]
    w_q = w_qkv[:, :HC * HDC].astype(BF16)
    w_k = w_qkv[:, HC * HDC:(HC + KVC) * HDC].astype(BF16)
    w_v = w_qkv[:, (HC + KVC) * HDC:].astype(BF16)
    (n1,) = rmsnorm(h, g_mix[1], [BF16])
    (sq,) = matmul([(n1, w_q)], [BF16], rope=(half_swa, tabs_swa), scale=swa_scale)
    sk32, sk = matmul([(n1, w_k)], [F32, BF16], rope=(half_swa, tabs_swa))
    sv32, sv = matmul([(n1, w_v)], [F32, BF16])
    sinks = c_sinks[0].astype(F32)
    o_swa_p = swa_prompt(p3(sq), p3(sk), p3(sv), sinks, LEAD)
    q_rows = jnp.transpose(sq[TP:].reshape(DB, DS, KVC, GQA, HDC), (0, 2, 3, 1, 4))
    q_rows = jnp.pad(q_rows.reshape(DB, KVC, GQA * DS, HDC), ((0, 0), (0, 0), (0, 0), (0, LANES - HDC)))
    sink_rows = jnp.broadcast_to(sinks.reshape(KVC, GQA, 1, 1), (KVC, GQA, DS, LANES)).reshape(KVC, GQA * DS, LANES)
    o_rows, swa_k_s, swa_v_s = swa_sample(
        q_rows, sink_rows, cache_swa_k[0].reshape(DB, WINDOW, KVC * HDC), cache_swa_v[0].reshape(DB, WINDOW, KVC * HDC),
        sk32[TP:].reshape(DB, DS, KVC * HDC), sv32[TP:].reshape(DB, DS, KVC * HDC))
    o_swa_s = jnp.transpose(o_rows[..., :HDC].reshape(DB, KVC, GQA, DS, HDC), (0, 3, 1, 2, 4)).reshape(TS, HC * HDC)
    o_swa = jnp.concatenate([o_swa_p.reshape(TP, HC * HDC), o_swa_s], axis=0)
    (h,) = matmul([(o_swa, c_w_o[0].astype(BF16))], [F32], res=h)
    h = _ffn(h, g_ffn[1], w_gate[1].astype(BF16), w_up[1].astype(BF16), w_down[1].astype(BF16))

    y_prompt = final_norm_prompt(h[:TP].reshape(B, LP, D), g_final, (LEAD + N_META) // BLOCK)
    (y_sample,) = rmsnorm(h[TP:], g_final, [F32])
    pv = lambda a, shape: a[:TP].reshape((B, LP) + shape)[:, LEAD:][None]
    sv_ = lambda a, shape: a[TP:].reshape((1, DB, DS) + shape)
    return (
        y_prompt, y_sample.reshape(DB, DS, D),
        pv(fk32, (HA, DA)), pv(fv32, (HA, DA)), pv(logf, (LANES,))[..., :HA],
        pv(ckv32, (KV_RANK,)), pv(kpe32, (LANES,))[..., :ROPE_D],
        pv(sk32, (KVC, HDC))[:, :, -WINDOW:], pv(sv32, (KVC, HDC))[:, :, -WINDOW:],
        sv_(fk32, (HA, DA)), sv_(fv32, (HA, DA)), sv_(logf, (LANES,))[..., :HA],
        sv_(ckv32, (KV_RANK,)), sv_(kpe32, (LANES,))[..., :ROPE_D],
        swa_k_s.reshape(1, DB, WINDOW, KVC, HDC), swa_v_s.reshape(1, DB, WINDOW, KVC, HDC),
    )
```
